```python
import math
import jax
import jax.numpy as jnp
from jax import lax
import numpy as np

D_MODEL = 1024
BATCH = 8
SEQ = 2048
DEPTH = 2

GRID_W = 64
CTX_LEN = 256
N_MIXERS = 2
N_MLA = (DEPTH + 1) // 2
N_HYENA = DEPTH // 2

MLA_HEADS = 16
QK_NOPE = 64
QK_ROPE = 32
QK_DIM = QK_NOPE + QK_ROPE
V_HEAD = 64
Q_LORA = 512
KV_LORA = 256
ROPE_THETA = 10000.0
Q_BLOCK = 128

HY_BANDS = 16
HY_POS_DIM = 1 + 2 * HY_BANDS
HY_FILTER_HIDDEN = 64
HY_DECAY_SLOW = math.log(100.0) / 1.5
HY_DECAY_FAST = math.log(100.0) / 0.3
HY_FILTER_INIT = 0.02

D_FF = 2816

EPS = 1e-6
F32 = jnp.float32

kernel_name = 'hybrid_mla_hyena_prefix_dit'


def rmsnorm(x, g):
    xf = x.astype(F32)
    y = xf * lax.rsqrt(jnp.mean(xf * xf, axis=-1, keepdims=True) + EPS)
    return (y * g.astype(F32)).astype(x.dtype)


def modulate(h, shift, scale):
    return h * (1 + scale) + shift


def dwconv3(x, w, b):
    xp = jnp.pad(x, ((0, 0), (1, 1), (0, 0)))
    return xp[:, :-2] * w[0] + xp[:, 1:-1] * w[1] + xp[:, 2:] * w[2] + b


def axial_rope_tables(rows):
    row = jnp.repeat(jnp.arange(rows, dtype=F32), GRID_W)
    col = jnp.tile(jnp.arange(GRID_W, dtype=F32), rows)
    half = QK_ROPE // 2
    inv_freq = ROPE_THETA ** (-jnp.arange(0, half, 2, dtype=F32) / half)
    ang_r = row[:, None] * inv_freq
    ang_c = col[:, None] * inv_freq
    ang = jnp.concatenate([ang_r, ang_r, ang_c, ang_c], axis=-1)
    return jnp.cos(ang), jnp.sin(ang)


def apply_axial_rope(x, cos, sin):
    q = QK_ROPE // 4
    xf = x.astype(F32)
    rot = jnp.concatenate([-xf[..., q:2 * q], xf[..., :q], -xf[..., 3 * q:], xf[..., 2 * q:3 * q]], axis=-1)
    return (xf * cos + rot * sin).astype(x.dtype)


def mla_queries(h, w_dq, g_q, w_uq, cos, sin):
    B, L, _ = h.shape
    cq = rmsnorm(h @ w_dq, g_q)
    q = (cq @ w_uq).reshape(B, L, MLA_HEADS, QK_DIM)
    q_nope, q_pe = q[..., :QK_NOPE], q[..., QK_NOPE:]
    if cos is not None:
        q_pe = apply_axial_rope(q_pe, cos[None, :, None, :], sin[None, :, None, :])
    return jnp.concatenate([q_nope, q_pe], axis=-1)


def mla_keys_values(h, w_dkv, g_kv, w_uk, w_uv, cos, sin):
    B, L, _ = h.shape
    kv = h @ w_dkv
    ckv = rmsnorm(kv[..., :KV_LORA], g_kv)
    k_pe = kv[..., KV_LORA:]
    if cos is not None:
        k_pe = apply_axial_rope(k_pe, cos[None], sin[None])
    k_nope = (ckv @ w_uk).reshape(B, L, MLA_HEADS, QK_NOPE)
    v = (ckv @ w_uv).reshape(B, L, MLA_HEADS, V_HEAD)
    k_pe = jnp.broadcast_to(k_pe[:, :, None, :], (B, L, MLA_HEADS, QK_ROPE))
    return jnp.concatenate([k_nope, k_pe], axis=-1), v


def block_attention(q, k, v):
    B, Lq, H, dk = q.shape
    nblk = Lq // Q_BLOCK
    scale = 1.0 / math.sqrt(dk)
    qb = q.reshape(B, nblk, Q_BLOCK, H, dk).transpose(1, 0, 2, 3, 4)

    def one_block(qblk):
        s = jnp.einsum('bqhd,bkhd->bhqk', qblk, k).astype(F32) * scale
        p = jax.nn.softmax(s, axis=-1).astype(v.dtype)
        return jnp.einsum('bhqk,bkhd->bqhd', p, v)

    o = lax.map(one_block, qb)
    return o.transpose(1, 0, 2, 3, 4).reshape(B, Lq, H * v.shape[-1])


def hyena_pos_features(L):
    t = jnp.linspace(0.0, 1.0, L, dtype=F32)
    w = 2.0 * math.pi * jnp.arange(L, dtype=F32) / L
    f = jnp.linspace(1e-4, HY_BANDS - 1, HY_BANDS, dtype=F32)
    z = jnp.concatenate([t[:, None], jnp.cos(w[:, None] * f), -jnp.sin(w[:, None] * f)], axis=-1)
    return t, z


def hyena_filters(L, f_w1, f_b1, f_freq1, f_w2, f_b2, f_freq2, f_w3, decay):
    t, z = hyena_pos_features(L)
    h = jnp.sin(f_freq1.astype(F32) * (z @ f_w1.astype(F32) + f_b1.astype(F32)))
    h = jnp.sin(f_freq2.astype(F32) * (h @ f_w2.astype(F32) + f_b2.astype(F32)))
    h = h @ f_w3.astype(F32)
    window = jnp.exp(-t[:, None] * jnp.abs(decay.astype(F32))[None, :])
    d = h.shape[1] // 2
    return h[:, :d] * window, h[:, d:] * window


def two_sided_fftconv(u, h_fwd, h_bwd):
    L = u.shape[1]
    filt = jnp.concatenate([h_fwd, jnp.zeros((1, h_fwd.shape[1]), F32), h_bwd[1:][::-1]], axis=0)
    filt_f = jnp.fft.rfft(filt, n=2 * L, axis=0)
    u_f = jnp.fft.rfft(u.astype(F32), n=2 * L, axis=1)
    y = jnp.fft.irfft(u_f * filt_f[None], n=2 * L, axis=1)[:, :L]
    return y.astype(u.dtype)


def hyena_mixer(u, w_in, b_in, conv_w, conv_b, f_w1, f_b1, f_freq1, f_w2, f_b2, f_freq2, f_w3, decay, d_bias, w_out, b_out):
    L = u.shape[1]
    z = dwconv3(u @ w_in + b_in, conv_w, conv_b)
    x1, x2, v = jnp.split(z, 3, axis=-1)
    h_fwd, h_bwd = hyena_filters(L, f_w1, f_b1, f_freq1, f_w2, f_b2, f_freq2, f_w3, decay)
    v = v * x2
    y = (two_sided_fftconv(v, h_fwd, h_bwd) + v * d_bias) * x1
    return y @ w_out + b_out


def conv_ffn(h, w_up, conv_w, conv_b, w_down):
    z = dwconv3(h @ w_up, conv_w, conv_b)
    a, g = jnp.split(z, 2, axis=-1)
    return (jax.nn.silu(g) * a) @ w_down


def setup_inputs(seed: int = 0) -> dict:
    key = jax.random.key(seed)
    ks = jax.random.split(key, 40)
    D = D_MODEL

    def nrm(i, shape, scale):
        return jax.random.normal(ks[i], shape, F32) * scale

    def gain(i, shape):
        return 1.0 + nrm(i, shape, 0.05)

    decay_base = jnp.linspace(HY_DECAY_SLOW, HY_DECAY_FAST, D, dtype=F32)
    return {
        'x': nrm(0, (BATCH, SEQ, D), 1.0),
        'c': nrm(1, (BATCH, D), 1.0),
        'ctx': nrm(2, (BATCH, CTX_LEN, D), 1.0),
        'c_ctx': nrm(3, (D,), 1.0),
        'mod_w': nrm(4, (DEPTH, D, 6 * D), D ** -0.5),
        'mod_b': nrm(5, (DEPTH, 6 * D), 0.02),
        'norm_mix_g': gain(6, (DEPTH, D)),
        'norm_ffn_g': gain(7, (DEPTH, D)),
        'mla_w_dq': nrm(8, (N_MLA, D, Q_LORA), D ** -0.5),
        'mla_g_q': gain(9, (N_MLA, Q_LORA)),
        'mla_w_uq': nrm(10, (N_MLA, Q_LORA, MLA_HEADS * QK_DIM), Q_LORA ** -0.5),
        'mla_w_dkv': nrm(11, (N_MLA, D, KV_LORA + QK_ROPE), D ** -0.5),
        'mla_g_kv': gain(12, (N_MLA, KV_LORA)),
        'mla_w_uk': nrm(13, (N_MLA, KV_LORA, MLA_HEADS * QK_NOPE), KV_LORA ** -0.5),
        'mla_w_uv': nrm(14, (N_MLA, KV_LORA, MLA_HEADS * V_HEAD), KV_LORA ** -0.5),
        'mla_w_o': nrm(15, (N_MLA, MLA_HEADS * V_HEAD, D), (MLA_HEADS * V_HEAD) ** -0.5),
        'hy_w_in': nrm(16, (N_HYENA, D, 3 * D), D ** -0.5),
        'hy_b_in': nrm(17, (N_HYENA, 3 * D), 0.02),
        'hy_conv_w': nrm(18, (N_HYENA, 3, 3 * D), 3 ** -0.5),
        'hy_conv_b': nrm(19, (N_HYENA, 3 * D), 0.02),
        'hy_f_w1': nrm(20, (N_HYENA, HY_POS_DIM, HY_FILTER_HIDDEN), HY_POS_DIM ** -0.5),
        'hy_f_b1': nrm(21, (N_HYENA, HY_FILTER_HIDDEN), 0.1),
        'hy_f_freq1': gain(22, (N_HYENA, HY_FILTER_HIDDEN)),
        'hy_f_w2': nrm(23, (N_HYENA, HY_FILTER_HIDDEN, HY_FILTER_HIDDEN), HY_FILTER_HIDDEN ** -0.5),
        'hy_f_b2': nrm(24, (N_HYENA, HY_FILTER_HIDDEN), 0.1),
        'hy_f_freq2': gain(25, (N_HYENA, HY_FILTER_HIDDEN)),
        'hy_f_w3': nrm(26, (N_HYENA, HY_FILTER_HIDDEN, 2 * D), HY_FILTER_INIT),
        'hy_decay': decay_base[None, :] * (1.0 + nrm(27, (N_HYENA, D), 0.05)),
        'hy_d_bias': nrm(28, (N_HYENA, D), 1.0),
        'hy_w_out': nrm(29, (N_HYENA, D, D), D ** -0.5),
        'hy_b_out': nrm(30, (N_HYENA, D), 0.02),
        'ffn_w_up': nrm(31, (DEPTH, D, 2 * D_FF), D ** -0.5),
        'ffn_conv_w': nrm(32, (DEPTH, 3, 2 * D_FF), 3 ** -0.5),
        'ffn_conv_b': nrm(33, (DEPTH, 2 * D_FF), 0.02),
        'ffn_w_down': nrm(34, (DEPTH, D_FF, D), D_FF ** -0.5),
        'final_g': gain(35, (D,)),
    }


def reference(x, c, ctx, c_ctx, mod_w, mod_b, norm_mix_g, norm_ffn_g,
              mla_w_dq, mla_g_q, mla_w_uq, mla_w_dkv, mla_g_kv, mla_w_uk, mla_w_uv, mla_w_o,
              hy_w_in, hy_b_in, hy_conv_w, hy_conv_b, hy_f_w1, hy_f_b1, hy_f_freq1, hy_f_w2, hy_f_b2,
              hy_f_freq2, hy_f_w3, hy_decay, hy_d_bias, hy_w_out, hy_b_out,
              ffn_w_up, ffn_conv_w, ffn_conv_b, ffn_w_down, final_g):
    L = x.shape[1]
    ROWS = L // GRID_W
    cos, sin = axial_rope_tables(ROWS)
    silu_c = jax.nn.silu(c)
    silu_cc = jax.nn.silu(c_ctx)
    h_ctx = ctx
    for i in range(DEPTH):
        last = i == DEPTH - 1
        j = i // N_MIXERS
        mod_x = silu_c @ mod_w[i] + mod_b[i]
        mod_c = silu_cc @ mod_w[i] + mod_b[i]
        sh1, sc1, g1, sh2, sc2, g2 = [m[:, None, :] for m in jnp.split(mod_x, 6, axis=-1)]
        csh1, csc1, cg1, csh2, csc2, cg2 = jnp.split(mod_c, 6, axis=-1)
        hx = modulate(rmsnorm(x, norm_mix_g[i]), sh1, sc1)
        if i % N_MIXERS == 0:
            hc = modulate(rmsnorm(h_ctx, norm_mix_g[i]), csh1, csc1)
            qx = mla_queries(hx, mla_w_dq[j], mla_g_q[j], mla_w_uq[j], cos, sin)
            kx, vx = mla_keys_values(hx, mla_w_dkv[j], mla_g_kv[j], mla_w_uk[j], mla_w_uv[j], cos, sin)
            kc, vc = mla_keys_values(hc, mla_w_dkv[j], mla_g_kv[j], mla_w_uk[j], mla_w_uv[j], None, None)
            ox = block_attention(qx, jnp.concatenate([kc, kx], axis=1), jnp.concatenate([vc, vx], axis=1))
            x = x + g1 * (ox @ mla_w_o[j])
            if not last:
                qc = mla_queries(hc, mla_w_dq[j], mla_g_q[j], mla_w_uq[j], None, None)
                oc = block_attention(qc, kc, vc)
                h_ctx = h_ctx + cg1 * (oc @ mla_w_o[j])
        else:
            hy = (hy_w_in[j], hy_b_in[j], hy_conv_w[j], hy_conv_b[j], hy_f_w1[j], hy_f_b1[j], hy_f_freq1[j],
                  hy_f_w2[j], hy_f_b2[j], hy_f_freq2[j], hy_f_w3[j], hy_decay[j], hy_d_bias[j],
                  hy_w_out[j], hy_b_out[j])
            x = x + g1 * hyena_mixer(hx, *hy)
            if not last:
                hc = modulate(rmsnorm(h_ctx, norm_mix_g[i]), csh1, csc1)
                h_ctx = h_ctx + cg1 * hyena_mixer(hc, *hy)
        ffn = (ffn_w_up[i], ffn_conv_w[i], ffn_conv_b[i], ffn_w_down[i])
        x = x + g2 * conv_ffn(modulate(rmsnorm(x, norm_ffn_g[i]), sh2, sc2), *ffn)
        if not last:
            h_ctx = h_ctx + cg2 * conv_ffn(modulate(rmsnorm(h_ctx, norm_ffn_g[i]), csh2, csc2), *ffn)
    return rmsnorm(x, final_g)
```

```python
import functools
import math

import numpy as np
import jax
import jax.numpy as jnp
from jax import lax
from jax.experimental import pallas as pl
from jax.experimental.pallas import tpu as pltpu

F32 = jnp.float32
BF16 = jnp.bfloat16

GRID_W = 64
MLA_HEADS = 16
QK_NOPE = 64
QK_ROPE = 32
QK_DIM = QK_NOPE + QK_ROPE
V_HEAD = 64
Q_LORA = 512
KV_LORA = 256
ROPE_THETA = 10000.0
HY_BANDS = 16
EPS = 1e-6

LANES = 128
BF16_ROWS = 16
MXU_DIM = 256
VMEM_LIMIT_BYTES = 56 * 1024 * 1024

HEAD_PAD = LANES
HALO = BF16_ROWS


def _rmsnorm(x, g):
    return x * lax.rsqrt(jnp.mean(x * x, axis=-1, keepdims=True) + EPS) * g


def _sigmoid(x):
    return 1.0 / (1.0 + jnp.exp(-x))


def _dot(a, b):
    return jnp.dot(a, b, preferred_element_type=F32)


def _dot_nt(a, b):
    return lax.dot_general(a, b, (((1,), (1,)), ((), ())), preferred_element_type=F32)


def _split_bf16(a):
    hi = a.astype(BF16)
    lo = (a - hi.astype(F32)).astype(BF16)
    return hi, lo


def _dot3(a, b):
    a_hi, a_lo = _split_bf16(a)
    b_hi, b_lo = _split_bf16(b)
    return _dot(a_hi, b_hi) + _dot(a_lo, b_hi) + _dot(a_hi, b_lo)


def _const_spec(shape):
    nd = len(shape)
    return pl.BlockSpec(shape, lambda *_: (0,) * nd, pipeline_mode=pl.Buffered(1))


def _params(n_axes):
    return pltpu.CompilerParams(dimension_semantics=("arbitrary",) * n_axes,
                                vmem_limit_bytes=VMEM_LIMIT_BYTES)


def _mod_kernel(cc_ref, w_ref, b_ref, o_ref):
    cc = cc_ref[...]
    s = cc * _sigmoid(cc)
    o_ref[0] = _dot3(s, w_ref[0]) + b_ref[0]


def _modulation(cc, mod_w, mod_b):
    depth, d, n6 = mod_w.shape
    r = cc.shape[0]
    nb = n6 // 4
    return pl.pallas_call(
        _mod_kernel,
        grid=(depth, n6 // nb),
        in_specs=[pl.BlockSpec((r, d), lambda i, j: (0, 0)),
                  pl.BlockSpec((1, d, nb), lambda i, j: (i, 0, j)),
                  pl.BlockSpec((1, 1, nb), lambda i, j: (i, 0, j))],
        out_specs=pl.BlockSpec((1, r, nb), lambda i, j: (i, 0, j)),
        out_shape=jax.ShapeDtypeStruct((depth, r, n6), F32),
        compiler_params=_params(2),
        name="modulation",
    )(cc, mod_w, mod_b.reshape(depth, 1, n6))


def _mla_proj_kernel(n_ctx_blocks, q_scale,
                     ctx_ref, x_ref, mod_ref, gmix_ref, wcat_ref, gq_ref, wuqt_ref, gkv_ref, wuk_ref, e_ref,
                     wuvt_ref, cosk_ref, sink_ref, cosq_ref, sinq_ref,
                     qt_ref, k_ref, vt_ref):
    j = pl.program_id(1)
    is_latent = j >= n_ctx_blocks
    xt = jnp.where(is_latent, x_ref[0], ctx_ref[0])
    shift = mod_ref[0, 0, 0:1, :]
    scale = mod_ref[0, 0, 1:2, :]
    h = (_rmsnorm(xt, gmix_ref[...]) * (1.0 + scale) + shift).astype(BF16)
    y = _dot(h, wcat_ref[...])
    o_kv = Q_LORA
    o_pe = Q_LORA + KV_LORA
    ckv = _rmsnorm(y[:, o_kv:o_pe], gkv_ref[...]).astype(BF16)
    kpe = (y[:, o_pe:o_pe + LANES] * cosk_ref[...] + y[:, o_pe + LANES:o_pe + 2 * LANES] * sink_ref[...]).astype(BF16)
    kall = _dot(ckv, wuk_ref[...]) + _dot(kpe, e_ref[...])
    vt = _dot_nt(wuvt_ref[...], ckv)
    for hd in range(MLA_HEADS):
        k_ref[0, hd] = kall[:, hd * HEAD_PAD:(hd + 1) * HEAD_PAD].astype(BF16)
        vt_ref[0, hd] = vt[hd * V_HEAD:(hd + 1) * V_HEAD, :].astype(BF16)

    @pl.when(is_latent)
    def _():
        cq = _rmsnorm(y[:, :Q_LORA], gq_ref[...]).astype(BF16)
        qt = _dot_nt(wuqt_ref[...], cq) * q_scale
        cosq = cosq_ref[...]
        sinq = sinq_ref[...]
        zeros = jnp.zeros((HEAD_PAD - QK_DIM, qt.shape[1]), BF16)
        for hd in range(MLA_HEADS):
            base = hd * HEAD_PAD
            rope = (qt[base + QK_NOPE:base + QK_DIM] * cosq
                    + qt[base + QK_DIM:base + QK_DIM + QK_ROPE] * sinq)
            qt_ref[0, hd, 0:QK_NOPE, :] = qt[base:base + QK_NOPE].astype(BF16)
            qt_ref[0, hd, QK_NOPE:QK_DIM, :] = rope.astype(BF16)
            qt_ref[0, hd, QK_DIM:HEAD_PAD, :] = zeros


def _rot_cols(w):
    q = QK_ROPE // 4
    return jnp.concatenate([-w[:, q:2 * q], w[:, :q], -w[:, 3 * q:], w[:, 2 * q:3 * q]], axis=1)


def _pad_cols(w, n):
    return jnp.pad(w, ((0, 0), (0, n - w.shape[1])))


def _rope_tables(seq):
    rows = seq // GRID_W
    row = jnp.repeat(jnp.arange(rows, dtype=F32), GRID_W)
    col = jnp.tile(jnp.arange(GRID_W, dtype=F32), rows)
    half = QK_ROPE // 2
    inv_freq = ROPE_THETA ** (-jnp.arange(0, half, 2, dtype=F32) / half)
    ang_r = row[:, None] * inv_freq
    ang_c = col[:, None] * inv_freq
    ang = jnp.concatenate([ang_r, ang_r, ang_c, ang_c], axis=-1)
    return jnp.cos(ang), jnp.sin(ang)


def _mla_projections(x, ctx, mod_lat, mod_ctx, g_mix, w_dq, g_q, w_uq, w_dkv, g_kv, w_uk, w_uv, tl):
    b, seq, d = x.shape
    lc = ctx.shape[1]
    lk = lc + seq
    h = MLA_HEADS
    assert lc % tl == 0 and seq % tl == 0
    ncb = lc // tl
    nblk = lk // tl

    w_pe = w_dkv[:, KV_LORA:]
    wcat = jnp.concatenate([w_dq, w_dkv[:, :KV_LORA], _pad_cols(w_pe, LANES), _pad_cols(_rot_cols(w_pe), LANES)],
                           axis=1).astype(BF16)
    uq = w_uq.reshape(Q_LORA, h, QK_DIM)
    uq_pe = uq[:, :, QK_NOPE:].reshape(Q_LORA * h, QK_ROPE)
    uq_rot = _rot_cols(uq_pe).reshape(Q_LORA, h, QK_ROPE)
    wuqt = jnp.concatenate([uq, uq_rot], axis=2).reshape(Q_LORA, h * HEAD_PAD).T.astype(BF16)
    uk = w_uk.reshape(KV_LORA, h, QK_NOPE)
    wuk = jnp.pad(uk, ((0, 0), (0, 0), (0, HEAD_PAD - QK_NOPE))).reshape(KV_LORA, h * HEAD_PAD).astype(BF16)
    e_np = np.zeros((LANES, h, HEAD_PAD), np.float32)
    for r in range(QK_ROPE):
        e_np[r, :, QK_NOPE + r] = 1.0
    e_mat = jnp.asarray(e_np.reshape(LANES, h * HEAD_PAD), BF16)
    wuvt = w_uv.T.astype(BF16)

    cos, sin = _rope_tables(seq)
    cosk = jnp.concatenate([jnp.ones((lc, LANES), F32), _pad_cols(cos, LANES)], axis=0)
    sink = jnp.concatenate([jnp.zeros((lc, LANES), F32), _pad_cols(sin, LANES)], axis=0)
    cosq = cos.T
    sinq = sin.T
    mod = jnp.concatenate([jnp.broadcast_to(mod_ctx[None, None], (b, 1, 2, d)), mod_lat[:, None]], axis=1)

    q_scale = math.log2(math.e) / math.sqrt(QK_DIM)
    lat_blk = lambda j: jnp.maximum(j - ncb, 0)
    kernel = functools.partial(_mla_proj_kernel, ncb, q_scale)
    return pl.pallas_call(
        kernel,
        grid=(b, nblk),
        in_specs=[
            pl.BlockSpec((1, tl, d), lambda i, j: (i, jnp.minimum(j, ncb - 1), 0)),
            pl.BlockSpec((1, tl, d), lambda i, j: (i, lat_blk(j), 0)),
            pl.BlockSpec((1, 1, 2, d), lambda i, j: (i, jnp.where(j >= ncb, 1, 0), 0, 0)),
            _const_spec((1, d)),
            _const_spec(wcat.shape),
            _const_spec((1, Q_LORA)),
            _const_spec(wuqt.shape),
            _const_spec((1, KV_LORA)),
            _const_spec(wuk.shape),
            _const_spec(e_mat.shape),
            _const_spec(wuvt.shape),
            pl.BlockSpec((tl, LANES), lambda i, j: (j, 0)),
            pl.BlockSpec((tl, LANES), lambda i, j: (j, 0)),
            pl.BlockSpec((QK_ROPE, tl), lambda i, j: (0, lat_blk(j))),
            pl.BlockSpec((QK_ROPE, tl), lambda i, j: (0, lat_blk(j))),
        ],
        out_specs=[
            pl.BlockSpec((1, h, HEAD_PAD, tl), lambda i, j: (i, 0, 0, lat_blk(j))),
            pl.BlockSpec((1, h, tl, HEAD_PAD), lambda i, j: (i, 0, j, 0)),
            pl.BlockSpec((1, h, V_HEAD, tl), lambda i, j: (i, 0, 0, j)),
        ],
        out_shape=[
            jax.ShapeDtypeStruct((b, h, HEAD_PAD, seq), BF16),
            jax.ShapeDtypeStruct((b, h, lk, HEAD_PAD), BF16),
            jax.ShapeDtypeStruct((b, h, V_HEAD, lk), BF16),
        ],
        compiler_params=_params(2),
        name="mla_proj",
    )(ctx, x, mod, g_mix.reshape(1, d), wcat, g_q.reshape(1, Q_LORA), wuqt, g_kv.reshape(1, KV_LORA), wuk, e_mat,
      wuvt, cosk, sink, cosq, sinq)


def _attn_kernel(kc, qt_ref, k_ref, vt_ref, o_ref, s_ref, ot_ref):
    lk = k_ref.shape[2]
    tq = qt_ref.shape[3]
    n_chunks = lk // kc
    sub = 8

    def head(hd, carry):
        qt = qt_ref[0, hd]
        mpart = jnp.full((sub, tq), -jnp.inf, F32)
        for c in range(n_chunks):
            s = _dot(k_ref[0, hd, c * kc:(c + 1) * kc, :], qt)
            s_ref[c * kc:(c + 1) * kc, :] = s
            mpart = jnp.maximum(mpart, jnp.max(s.reshape(kc // sub, sub, tq), axis=0))
        m = jnp.max(mpart, axis=0, keepdims=True)
        lpart = jnp.zeros((sub, tq), F32)
        acc = jnp.zeros((V_HEAD, tq), F32)
        for c in range(n_chunks):
            p = jnp.exp2(s_ref[c * kc:(c + 1) * kc, :] - m)
            lpart = lpart + jnp.sum(p.reshape(kc // sub, sub, tq), axis=0)
            acc = acc + _dot(vt_ref[0, hd, :, c * kc:(c + 1) * kc], p.astype(BF16))
        denom = jnp.sum(lpart, axis=0, keepdims=True)
        ot_ref[pl.ds(pl.multiple_of(hd * V_HEAD, V_HEAD), V_HEAD), :] = acc / denom
        return carry

    lax.fori_loop(0, MLA_HEADS, head, 0)
    o_ref[0] = ot_ref[...].T.astype(BF16)


def _attention(qt, k, vt, tq, kc):
    b, h, _, seq = qt.shape
    lk = k.shape[2]
    assert seq % tq == 0 and lk % kc == 0
    return pl.pallas_call(
        functools.partial(_attn_kernel, kc),
        grid=(b, seq // tq),
        in_specs=[pl.BlockSpec((1, h, HEAD_PAD, tq), lambda i, j: (i, 0, 0, j)),
                  pl.BlockSpec((1, h, lk, HEAD_PAD), lambda i, j: (i, 0, 0, 0)),
                  pl.BlockSpec((1, h, V_HEAD, lk), lambda i, j: (i, 0, 0, 0))],
        out_specs=pl.BlockSpec((1, tq, h * V_HEAD), lambda i, j: (i, j, 0)),
        out_shape=jax.ShapeDtypeStruct((b, seq, h * V_HEAD), BF16),
        scratch_shapes=[pltpu.VMEM((lk, tq), F32), pltpu.VMEM((h * V_HEAD, tq), F32)],
        compiler_params=_params(2),
        name="mla_attention",
    )(qt, k, vt)


def _halo_specs(tl, d, seq):
    r = tl // HALO
    last = seq // HALO - 1

    def prev_map(i, j):
        return (i, jnp.maximum(j * r - 1, 0), 0)

    def next_map(i, j):
        return (i, jnp.minimum((j + 1) * r, last), 0)

    return [pl.BlockSpec((1, HALO, d), prev_map),
            pl.BlockSpec((1, tl, d), lambda i, j: (i, j, 0)),
            pl.BlockSpec((1, HALO, d), next_map)]


def _halo_rows_valid(n_rows):
    j = pl.program_id(1)
    row = lax.broadcasted_iota(jnp.int32, (n_rows, 1), 0)
    first = j == 0
    last = j == pl.num_programs(1) - 1
    return jnp.logical_not((first & (row < HALO)) | (last & (row >= n_rows - HALO)))


def _dwconv3(z, w_ref, b_ref, lo, hi, tl):
    n_rows = z.shape[0]
    zm = pltpu.roll(z, 1, axis=0)[HALO:HALO + tl]
    zp = pltpu.roll(z, n_rows - 1, axis=0)[HALO:HALO + tl]
    return (zm * w_ref[0:1, lo:hi] + z[HALO:HALO + tl] * w_ref[1:2, lo:hi] + zp * w_ref[2:3, lo:hi]
            + b_ref[0:1, lo:hi])


def _ffn_kernel(fc, final,
                xp_ref, xm_ref, xn_ref, mp_ref, mm_ref, mn_ref, mod_ref, wproj_ref, bproj_ref, gffn_ref,
                wup_ref, cw_ref, cb_ref, wdown_ref, gfin_ref, o_ref):
    tl = xm_ref.shape[1]
    ff = wdown_ref.shape[0]
    n_rows = tl + 2 * HALO
    g1 = mod_ref[0, 0:1, :]
    sh2 = mod_ref[0, 1:2, :]
    sc2 = mod_ref[0, 2:3, :]
    g2 = mod_ref[0, 3:4, :]
    xc = jnp.concatenate([xp_ref[0], xm_ref[0], xn_ref[0]], axis=0)
    mix = jnp.concatenate([mp_ref[0], mm_ref[0], mn_ref[0]], axis=0)
    xmid = xc + g1 * (_dot(mix, wproj_ref[...]) + bproj_ref[...])
    h = _rmsnorm(xmid, gffn_ref[...]) * (1.0 + sc2) + sh2
    h = jnp.where(_halo_rows_valid(n_rows), h, 0.0).astype(BF16)
    acc = jnp.zeros((tl, xm_ref.shape[2]), F32)
    for c in range(ff // fc):
        lo = c * fc
        za = _dot(h, wup_ref[:, lo:lo + fc])
        zg = _dot(h, wup_ref[:, ff + lo:ff + lo + fc])
        a = _dwconv3(za, cw_ref, cb_ref, lo, lo + fc, tl)
        g = _dwconv3(zg, cw_ref, cb_ref, ff + lo, ff + lo + fc, tl)
        act = (g * _sigmoid(g) * a).astype(BF16)
        acc = acc + _dot(act, wdown_ref[lo:lo + fc, :])
    out = xmid[HALO:HALO + tl] + g2 * acc
    if final:
        out = _rmsnorm(out, gfin_ref[...])
    o_ref[0] = out


def _ffn_layer(x, mix, mod4, w_proj, b_proj, g_ffn, w_up, conv_w, conv_b, w_down, g_final, final, tl, fc):
    b, seq, d = x.shape
    dm = mix.shape[2]
    ff = w_down.shape[0]
    assert seq % tl == 0 and tl % HALO == 0 and ff % fc == 0
    return pl.pallas_call(
        functools.partial(_ffn_kernel, fc, final),
        grid=(b, seq // tl),
        in_specs=_halo_specs(tl, d, seq) + _halo_specs(tl, dm, seq) + [
            pl.BlockSpec((1, 4, d), lambda i, j: (i, 0, 0)),
            _const_spec((dm, d)),
            _const_spec((1, d)),
            _const_spec((1, d)),
            _const_spec((d, 2 * ff)),
            _const_spec((3, 2 * ff)),
            _const_spec((1, 2 * ff)),
            _const_spec((ff, d)),
            _const_spec((1, d)),
        ],
        out_specs=pl.BlockSpec((1, tl, d), lambda i, j: (i, j, 0)),
        out_shape=jax.ShapeDtypeStruct((b, seq, d), F32),
        compiler_params=_params(2),
        name="mixer_out_ffn",
    )(x, x, x, mix, mix, mix, mod4, w_proj.astype(BF16), b_proj.reshape(1, d), g_ffn.reshape(1, d),
      w_up.astype(BF16), conv_w, conv_b.reshape(1, 2 * ff), w_down.astype(BF16), g_final.reshape(1, d))


def _transform_matrices(seq):
    n = 2 * seq
    blk = 64
    k = np.arange(seq, dtype=np.float64)[:, None] + 0.5
    w = 2.0 * np.pi * k / n
    n1 = np.arange(seq // blk, dtype=np.float64)[None, :] * blk
    n2 = np.arange(blk, dtype=np.float64)[None, :] + 0.5
    ca, sa = jnp.asarray(np.cos(w * n1), F32), jnp.asarray(np.sin(w * n1), F32)
    cb, sb = jnp.asarray(np.cos(w * n2), F32), jnp.asarray(np.sin(w * n2), F32)
    cmat = (ca[:, :, None] * cb[:, None, :] - sa[:, :, None] * sb[:, None, :]).reshape(seq, seq)
    smat = (sa[:, :, None] * cb[:, None, :] + ca[:, :, None] * sb[:, None, :]).reshape(seq, seq)
    return cmat.astype(BF16), smat.astype(BF16)


def _hyena_pos_features(seq):
    t = jnp.linspace(0.0, 1.0, seq, dtype=F32)
    w = 2.0 * math.pi * jnp.arange(seq, dtype=F32) / seq
    f = jnp.linspace(1e-4, HY_BANDS - 1, HY_BANDS, dtype=F32)
    z = jnp.concatenate([t[:, None], jnp.cos(w[:, None] * f), -jnp.sin(w[:, None] * f)], axis=-1)
    return t, z


def _hyena_filter_kernel(spec_scale, z_ref, t_ref, ph_ref, w1_ref, b1_ref, f1_ref, w2_ref, b2_ref, f2_ref,
                         w3f_ref, w3b_ref, decay_ref, cmat_ref, smat_ref, hr_ref, hi_ref):
    seq = z_ref.shape[0]
    hm = jnp.sin(f1_ref[...] * (_dot3(z_ref[...], w1_ref[...]) + b1_ref[...]))
    hm = jnp.sin(f2_ref[...] * (_dot3(hm, w2_ref[...]) + b2_ref[...]))
    window = jnp.exp(-t_ref[...] * jnp.abs(decay_ref[...]))
    h_fwd = _dot3(hm, w3f_ref[...]) * window
    h_bwd = _dot3(hm, w3b_ref[...]) * window
    row = lax.broadcasted_iota(jnp.int32, (seq, 1), 0)
    h_bwd = jnp.where(row == 0, 0.0, h_bwd)
    p = (h_fwd + h_bwd).astype(BF16)
    q = (h_fwd - h_bwd).astype(BF16)
    cph = ph_ref[:, 0:1] * spec_scale
    sph = ph_ref[:, 1:2] * spec_scale
    cmat = cmat_ref[...]
    smat = smat_ref[...]
    hr_ref[...] = cph * _dot(cmat, p) + sph * _dot(smat, p)
    hi_ref[...] = sph * _dot(cmat, q) - cph * _dot(smat, q)


def _hyena_spectra(seq, f_w1, f_b1, f_freq1, f_w2, f_b2, f_freq2, f_w3, decay, cmat, smat, dc):
    d = decay.shape[0]
    nh = f_w1.shape[1]
    t, z = _hyena_pos_features(seq)
    z = _pad_cols(z, LANES)
    f_w1 = jnp.pad(f_w1, ((0, LANES - f_w1.shape[0]), (0, 0)))
    n = 2 * seq
    kk = np.arange(seq, dtype=np.float64) + 0.5
    phase = jnp.asarray(np.stack([np.cos(np.pi * kk / n), np.sin(np.pi * kk / n)], axis=1), F32)
    nd = d // dc
    return pl.pallas_call(
        functools.partial(_hyena_filter_kernel, 2.0 / n),
        grid=(nd,),
        in_specs=[_const_spec(z.shape), _const_spec((seq, 1)), _const_spec((seq, 2)),
                  _const_spec(f_w1.shape), _const_spec((1, nh)), _const_spec((1, nh)),
                  _const_spec(f_w2.shape), _const_spec((1, nh)), _const_spec((1, nh)),
                  pl.BlockSpec((nh, dc), lambda j: (0, j)),
                  pl.BlockSpec((nh, dc), lambda j: (0, nd + j)),
                  pl.BlockSpec((1, dc), lambda j: (0, j)),
                  _const_spec((seq, seq)), _const_spec((seq, seq))],
        out_specs=[pl.BlockSpec((seq, dc), lambda j: (0, j)), pl.BlockSpec((seq, dc), lambda j: (0, j))],
        out_shape=[jax.ShapeDtypeStruct((seq, d), F32), jax.ShapeDtypeStruct((seq, d), F32)],
        compiler_params=_params(1),
        name="hyena_spectra",
    )(z, t.reshape(seq, 1), phase, f_w1, f_b1.reshape(1, nh), f_freq1.reshape(1, nh), f_w2, f_b2.reshape(1, nh),
      f_freq2.reshape(1, nh), f_w3, f_w3, decay.reshape(1, d), cmat, smat)


def _hyena_in_kernel(dc, xp_ref, xm_ref, xn_ref, mod_ref, gmix_ref, win_ref, bin_ref, cw_ref, cb_ref,
                     vv_ref, x1_ref):
    tl = xm_ref.shape[1]
    d = xm_ref.shape[2]
    n_rows = tl + 2 * HALO
    shift = mod_ref[0, 0:1, :]
    scale = mod_ref[0, 1:2, :]
    xc = jnp.concatenate([xp_ref[0], xm_ref[0], xn_ref[0]], axis=0)
    h = (_rmsnorm(xc, gmix_ref[...]) * (1.0 + scale) + shift).astype(BF16)
    valid = _halo_rows_valid(n_rows)
    for c in range(d // dc):
        parts = []
        for part in range(3):
            lo = part * d + c * dc
            z = _dot(h, win_ref[:, lo:lo + dc]) + bin_ref[0:1, lo:lo + dc]
            z = jnp.where(valid, z, 0.0)
            parts.append(_dwconv3(z, cw_ref, cb_ref, lo, lo + dc, tl))
        x1, x2, v = parts
        vv_ref[0, :, c * dc:(c + 1) * dc] = (v * x2).astype(BF16)
        x1_ref[0, :, c * dc:(c + 1) * dc] = x1.astype(BF16)


def _hyena_in(x, mod2, g_mix, w_in, b_in, conv_w, conv_b, tl, dc):
    b, seq, d = x.shape
    assert seq % tl == 0 and d % dc == 0
    out_spec = pl.BlockSpec((1, tl, d), lambda i, j: (i, j, 0))
    return pl.pallas_call(
        functools.partial(_hyena_in_kernel, dc),
        grid=(b, seq // tl),
        in_specs=_halo_specs(tl, d, seq) + [
            pl.BlockSpec((1, 2, d), lambda i, j: (i, 0, 0)),
            _const_spec((1, d)),
            _const_spec((d, 3 * d)),
            _const_spec((1, 3 * d)),
            _const_spec((3, 3 * d)),
            _const_spec((1, 3 * d)),
        ],
        out_specs=[out_spec, out_spec],
        out_shape=[jax.ShapeDtypeStruct((b, seq, d), BF16), jax.ShapeDtypeStruct((b, seq, d), BF16)],
        compiler_params=_params(2),
        name="hyena_in",
    )(x, x, x, mod2, g_mix.reshape(1, d), w_in.astype(BF16), b_in.reshape(1, 3 * d), conv_w, conv_b.reshape(1, 3 * d))


def _hyena_conv_kernel(vv_ref, x1_ref, hr_ref, hi_ref, dbias_ref, cmat_ref, smat_ref, o_ref):
    v = vv_ref[0]
    cmat = cmat_ref[...]
    smat = smat_ref[...]
    xc = _dot(cmat, v)
    xs = _dot(smat, v)
    hr = hr_ref[...]
    hi = hi_ref[...]
    yr = (xc * hr + xs * hi).astype(BF16)
    yi = (xc * hi - xs * hr).astype(BF16)
    y = _dot(cmat, yr) - _dot(smat, yi)
    o_ref[0] = ((y + v.astype(F32) * dbias_ref[...]) * x1_ref[0].astype(F32)).astype(BF16)


def _hyena_conv(vv, x1, hr, hi, d_bias, cmat, smat, dc):
    b, seq, d = vv.shape
    blk = pl.BlockSpec((1, seq, dc), lambda i, j: (i, 0, j))
    spec = pl.BlockSpec((seq, dc), lambda i, j: (0, j))
    return pl.pallas_call(
        _hyena_conv_kernel,
        grid=(b, d // dc),
        in_specs=[blk, blk, spec, spec, pl.BlockSpec((1, dc), lambda i, j: (0, j)),
                  _const_spec((seq, seq)), _const_spec((seq, seq))],
        out_specs=blk,
        out_shape=jax.ShapeDtypeStruct((b, seq, d), BF16),
        compiler_params=_params(2),
        name="hyena_conv",
    )(vv, x1, hr, hi, d_bias.reshape(1, d), cmat, smat)


def _tiles(seq, lc):
    proj_tl = math.gcd(MXU_DIM, math.gcd(seq, lc))
    return dict(proj_tl=proj_tl, attn_tq=min(MXU_DIM, seq), attn_kc=proj_tl,
                ffn_tl=min(2 * MXU_DIM, seq), ffn_fc=MXU_DIM, hy_tl=min(2 * MXU_DIM, seq), hy_dc=MXU_DIM)


def kernel(x, c, ctx, c_ctx, mod_w, mod_b, norm_mix_g, norm_ffn_g, mla_w_dq, mla_g_q, mla_w_uq, mla_w_dkv, mla_g_kv, mla_w_uk, mla_w_uv, mla_w_o, hy_w_in, hy_b_in, hy_conv_w, hy_conv_b, hy_f_w1, hy_f_b1, hy_f_freq1, hy_f_w2, hy_f_b2, hy_f_freq2, hy_f_w3, hy_decay, hy_d_bias, hy_w_out, hy_b_out, ffn_w_up, ffn_conv_w, ffn_conv_b, ffn_w_down, final_g):
    b, seq, d = x.shape
    lc = ctx.shape[1]
    depth = mod_w.shape[0]
    assert depth == 2, "layer 0 is MLA, layer 1 is Hyena"
    t = _tiles(seq, lc)

    rows = -(-(b + 1) // 8) * 8
    cc = jnp.concatenate([c, c_ctx[None], jnp.zeros((rows - b - 1, d), F32)], axis=0)
    mods = _modulation(cc, mod_w, mod_b).reshape(depth, rows, 6, d)

    m0 = mods[0]
    qt, k, vt = _mla_projections(x, ctx, m0[:b, 0:2], m0[b, 0:2], norm_mix_g[0], mla_w_dq[0], mla_g_q[0],
                                 mla_w_uq[0], mla_w_dkv[0], mla_g_kv[0], mla_w_uk[0], mla_w_uv[0], t["proj_tl"])
    ox = _attention(qt, k, vt, t["attn_tq"], t["attn_kc"])
    x = _ffn_layer(x, ox, m0[:b, 2:6], mla_w_o[0], jnp.zeros((d,), F32), norm_ffn_g[0], ffn_w_up[0], ffn_conv_w[0],
                   ffn_conv_b[0], ffn_w_down[0], final_g, False, t["ffn_tl"], t["ffn_fc"])

    m1 = mods[1]
    cmat, smat = _transform_matrices(seq)
    hr, hi = _hyena_spectra(seq, hy_f_w1[0], hy_f_b1[0], hy_f_freq1[0], hy_f_w2[0], hy_f_b2[0], hy_f_freq2[0],
                            hy_f_w3[0], hy_decay[0], cmat, smat, t["hy_dc"])
    vv, x1 = _hyena_in(x, m1[:b, 0:2], norm_mix_g[1], hy_w_in[0], hy_b_in[0], hy_conv_w[0], hy_conv_b[0],
                       t["hy_tl"], t["hy_dc"])
    yy = _hyena_conv(vv, x1, hr, hi, hy_d_bias[0], cmat, smat, t["hy_dc"])
    x = _ffn_layer(x, yy, m1[:b, 2:6], hy_w_out[0], hy_b_out[0], norm_ffn_g[1], ffn_w_up[1], ffn_conv_w[1],
                   ffn_conv_b[1], ffn_w_down[1], final_g, True, t["ffn_tl"], t["ffn_fc"])
    return x
```

```python
import functools
import math

import numpy as np
import jax
import jax.numpy as jnp
from jax import lax
from jax.experimental import pallas as pl
from jax.experimental.pallas import tpu as pltpu

F32 = jnp.float32
BF16 = jnp.bfloat16

GRID_W = 64
MLA_HEADS = 16
QK_NOPE = 64
QK_ROPE = 32
QK_DIM = QK_NOPE + QK_ROPE
V_HEAD = 64
Q_LORA = 512
KV_LORA = 256
ROPE_THETA = 10000.0
HY_BANDS = 16
EPS = 1e-6

LANES = 128
BF16_ROWS = 16
MXU_DIM = 256
VMEM_LIMIT_BYTES = 56 * 1024 * 1024

HEAD_PAD = LANES
HALO = BF16_ROWS
CONV_PAD = 8


def _rmsnorm(x, g):
    return x * lax.rsqrt(jnp.mean(x * x, axis=-1, keepdims=True) + EPS) * g


def _sigmoid(x):
    return 1.0 / (1.0 + jnp.exp(-x))


def _dot(a, b):
    return jnp.dot(a, b, preferred_element_type=F32)


def _dot_nt(a, b):
    return lax.dot_general(a, b, (((1,), (1,)), ((), ())), preferred_element_type=F32)


def _split_bf16(a):
    hi = a.astype(BF16)
    lo = (a - hi.astype(F32)).astype(BF16)
    return hi, lo


def _dot3(a, b):
    a_hi, a_lo = _split_bf16(a)
    b_hi, b_lo = _split_bf16(b)
    return _dot(a_hi, b_hi) + _dot(a_lo, b_hi) + _dot(a_hi, b_lo)


def _const_spec(shape):
    nd = len(shape)
    return pl.BlockSpec(shape, lambda *_: (0,) * nd, pipeline_mode=pl.Buffered(1))


def _params(n_axes):
    return pltpu.CompilerParams(dimension_semantics=("arbitrary",) * n_axes, vmem_limit_bytes=VMEM_LIMIT_BYTES)


def _mod_kernel(cc_ref, w_ref, b_ref, wup_ref, wdown_ref, o_ref, wup_bf_ref, wdown_bf_ref):
    cc = cc_ref[...]
    r = cc.shape[0]
    s_hi, s_lo = _split_bf16(cc * _sigmoid(cc))
    w_hi, w_lo = _split_bf16(w_ref[0])
    both = _dot(jnp.concatenate([s_hi, s_lo], axis=0), w_hi)
    o_ref[0] = both[:r] + both[r:] + _dot(s_hi, w_lo) + b_ref[0]
    wup_bf_ref[...] = wup_ref[...].astype(BF16)
    wdown_bf_ref[...] = wdown_ref[...].astype(BF16)


MOD_BLOCKS = 4


def _modulation(cc, mod_w, mod_b, ffn_w_up, ffn_w_down):
    depth, d, n6 = mod_w.shape
    r = cc.shape[0]
    nb = n6 // MOD_BLOCKS
    _, _, f2 = ffn_w_up.shape
    _, ff, _ = ffn_w_down.shape
    assert f2 % (MOD_BLOCKS * LANES) == 0 and ff % (MOD_BLOCKS * BF16_ROWS) == 0
    up_spec = pl.BlockSpec((1, d, f2 // MOD_BLOCKS), lambda i, j: (i, 0, j))
    down_spec = pl.BlockSpec((1, ff // MOD_BLOCKS, d), lambda i, j: (i, j, 0))
    return pl.pallas_call(
        _mod_kernel,
        grid=(depth, MOD_BLOCKS),
        in_specs=[pl.BlockSpec((r, d), lambda i, j: (0, 0)),
                  pl.BlockSpec((1, d, nb), lambda i, j: (i, 0, j)),
                  pl.BlockSpec((1, 1, nb), lambda i, j: (i, 0, j)),
                  up_spec, down_spec],
        out_specs=[pl.BlockSpec((1, r, nb), lambda i, j: (i, 0, j)), up_spec, down_spec],
        out_shape=[jax.ShapeDtypeStruct((depth, r, n6), F32),
                   jax.ShapeDtypeStruct(ffn_w_up.shape, BF16), jax.ShapeDtypeStruct(ffn_w_down.shape, BF16)],
        compiler_params=_params(2),
        name="modulation",
    )(cc, mod_w, mod_b.reshape(depth, 1, n6), ffn_w_up, ffn_w_down)


def _mla_kv_kernel(x_ref, mod_ref, gmix_ref, wcat_ref, gkv_ref, wuk_ref, wuvt_ref, k_ref, vt_ref):
    h = (_rmsnorm(x_ref[0], gmix_ref[...]) * (1.0 + mod_ref[1:2, :]) + mod_ref[0:1, :]).astype(BF16)
    y = _dot(h, wcat_ref[...])
    ckv = _rmsnorm(y[:, :KV_LORA], gkv_ref[...]).astype(BF16)
    _store_kv(ckv, y[:, KV_LORA:KV_LORA + LANES], wuk_ref, wuvt_ref, k_ref, vt_ref)


def _store_kv(ckv, kpe, wuk_ref, wuvt_ref, k_ref, vt_ref):
    knope = _dot(ckv, wuk_ref[...])
    vt = _dot_nt(wuvt_ref[...], ckv)
    for hd in range(MLA_HEADS):
        k_ref[0, hd] = (knope[:, hd * HEAD_PAD:(hd + 1) * HEAD_PAD] + kpe).astype(BF16)
        vt_ref[0, hd] = vt[hd * V_HEAD:(hd + 1) * V_HEAD, :].astype(BF16)


def _mla_proj_kernel(q_scale, x_ref, mod_ref, gmix_ref, wcat_ref, gq_ref, wuqt_ref, gkv_ref, wuk_ref,
                     wuvt_ref, cosk_ref, sink_ref, cosq_ref, sinq_ref, qt_ref, k_ref, vt_ref):
    h = (_rmsnorm(x_ref[0], gmix_ref[...]) * (1.0 + mod_ref[0, 1:2, :]) + mod_ref[0, 0:1, :]).astype(BF16)
    y = _dot(h, wcat_ref[...])
    o_kv = Q_LORA
    o_pe = Q_LORA + KV_LORA
    ckv = _rmsnorm(y[:, o_kv:o_pe], gkv_ref[...]).astype(BF16)
    kpe = y[:, o_pe:o_pe + LANES] * cosk_ref[...] + y[:, o_pe + LANES:o_pe + 2 * LANES] * sink_ref[...]
    _store_kv(ckv, kpe, wuk_ref, wuvt_ref, k_ref, vt_ref)

    cq = _rmsnorm(y[:, :Q_LORA], gq_ref[...]).astype(BF16)
    qt = _dot_nt(wuqt_ref[...], cq) * q_scale
    cosq = cosq_ref[...]
    sinq = sinq_ref[...]
    zeros = jnp.zeros((HEAD_PAD - QK_DIM, qt.shape[1]), BF16)
    q4 = QK_ROPE // 4
    for hd in range(MLA_HEADS):
        base = hd * QK_DIM
        pe = qt[base + QK_NOPE:base + QK_DIM]
        swapped = jnp.concatenate([pe[q4:2 * q4], pe[:q4], pe[3 * q4:], pe[2 * q4:3 * q4]], axis=0)
        qt_ref[0, hd, 0:QK_NOPE, :] = qt[base:base + QK_NOPE].astype(BF16)
        qt_ref[0, hd, QK_NOPE:QK_DIM, :] = (pe * cosq + swapped * sinq).astype(BF16)
        qt_ref[0, hd, QK_DIM:HEAD_PAD, :] = zeros


def _rot_cols(w):
    q = QK_ROPE // 4
    return jnp.concatenate([-w[:, q:2 * q], w[:, :q], -w[:, 3 * q:], w[:, 2 * q:3 * q]], axis=1)


def _pad_cols(w, n):
    return jnp.pad(w, ((0, 0), (0, n - w.shape[1])))


def _rope_lanes(w):
    return jnp.pad(w, ((0, 0), (QK_NOPE, LANES - QK_DIM)))


def _rope_tables(seq):
    rows = seq // GRID_W
    row = jnp.repeat(jnp.arange(rows, dtype=F32), GRID_W)
    col = jnp.tile(jnp.arange(GRID_W, dtype=F32), rows)
    half = QK_ROPE // 2
    inv_freq = ROPE_THETA ** (-jnp.arange(0, half, 2, dtype=F32) / half)
    ang_r = row[:, None] * inv_freq
    ang_c = col[:, None] * inv_freq
    ang = jnp.concatenate([ang_r, ang_r, ang_c, ang_c], axis=-1)
    return jnp.cos(ang), jnp.sin(ang)


def _mla_projections(x, ctx, mod_lat, mod_ctx, g_mix, w_dq, g_q, w_uq, w_dkv, g_kv, w_uk, w_uv, tl, tl_ctx):
    b, seq, d = x.shape
    lc = ctx.shape[1]
    h = MLA_HEADS
    assert lc % tl_ctx == 0 and seq % tl == 0

    w_pe = w_dkv[:, KV_LORA:]
    w_kv = w_dkv[:, :KV_LORA]
    wcat = jnp.concatenate([w_dq, w_kv, _rope_lanes(w_pe), _rope_lanes(_rot_cols(w_pe))],
                           axis=1).astype(BF16)
    wcat_ctx = jnp.concatenate([w_kv, _rope_lanes(w_pe)], axis=1).astype(BF16)
    wuqt = w_uq.T.astype(BF16)
    uk = w_uk.reshape(KV_LORA, h, QK_NOPE)
    wuk = jnp.pad(uk, ((0, 0), (0, 0), (0, HEAD_PAD - QK_NOPE))).reshape(KV_LORA, h * HEAD_PAD).astype(BF16)
    wuvt = w_uv.T.astype(BF16)

    cos, sin = _rope_tables(seq)
    q4 = QK_ROPE // 4
    rot_sign = jnp.asarray(np.repeat([-1.0, 1.0, -1.0, 1.0], q4), F32)
    q_scale = math.log2(math.e) / math.sqrt(QK_DIM)

    def kv_specs(n, t):
        return ([pl.BlockSpec((1, h, t, HEAD_PAD), lambda i, j: (i, 0, j, 0)),
                 pl.BlockSpec((1, h, V_HEAD, t), lambda i, j: (i, 0, 0, j))],
                [jax.ShapeDtypeStruct((b, h, n, HEAD_PAD), BF16), jax.ShapeDtypeStruct((b, h, V_HEAD, n), BF16)])

    kv_out_specs, kv_out_shape = kv_specs(seq, tl)
    qt, k, vt = pl.pallas_call(
        functools.partial(_mla_proj_kernel, q_scale),
        grid=(b, seq // tl),
        in_specs=[
            pl.BlockSpec((1, tl, d), lambda i, j: (i, j, 0)),
            pl.BlockSpec((1, 2, d), lambda i, j: (i, 0, 0)),
            _const_spec((1, d)),
            _const_spec(wcat.shape),
            _const_spec((1, Q_LORA)),
            _const_spec(wuqt.shape),
            _const_spec((1, KV_LORA)),
            _const_spec(wuk.shape),
            _const_spec(wuvt.shape),
            pl.BlockSpec((tl, LANES), lambda i, j: (j, 0)),
            pl.BlockSpec((tl, LANES), lambda i, j: (j, 0)),
            pl.BlockSpec((QK_ROPE, tl), lambda i, j: (0, j)),
            pl.BlockSpec((QK_ROPE, tl), lambda i, j: (0, j)),
        ],
        out_specs=[pl.BlockSpec((1, h, HEAD_PAD, tl), lambda i, j: (i, 0, 0, j))] + kv_out_specs,
        out_shape=[jax.ShapeDtypeStruct((b, h, HEAD_PAD, seq), BF16)] + kv_out_shape,
        compiler_params=_params(2),
        name="mla_proj",
    )(x, mod_lat, g_mix.reshape(1, d), wcat, g_q.reshape(1, Q_LORA), wuqt, g_kv.reshape(1, KV_LORA), wuk, wuvt,
      _rope_lanes(cos), _rope_lanes(sin), cos.T, (sin * rot_sign).T)

    ctx_out_specs, ctx_out_shape = kv_specs(lc, tl_ctx)
    k_ctx, vt_ctx = pl.pallas_call(
        _mla_kv_kernel,
        grid=(b, lc // tl_ctx),
        in_specs=[
            pl.BlockSpec((1, tl_ctx, d), lambda i, j: (i, j, 0)),
            _const_spec((2, d)),
            _const_spec((1, d)),
            _const_spec(wcat_ctx.shape),
            _const_spec((1, KV_LORA)),
            _const_spec(wuk.shape),
            _const_spec(wuvt.shape),
        ],
        out_specs=ctx_out_specs,
        out_shape=ctx_out_shape,
        compiler_params=_params(2),
        name="mla_ctx_kv",
    )(ctx, mod_ctx, g_mix.reshape(1, d), wcat_ctx, g_kv.reshape(1, KV_LORA), wuk, wuvt)
    return qt, k, vt, k_ctx, vt_ctx


def _attn_kernel(kc, qt_ref, kx_ref, vtx_ref, kc_ref, vtc_ref, o_ref, s0_ref, s1_ref, ot_ref):
    tq = qt_ref.shape[3]
    sub = 8
    chunks = []
    for k_src, vt_src in ((kc_ref, vtc_ref), (kx_ref, vtx_ref)):
        n = k_src.shape[2]
        step = math.gcd(kc, n)
        base = chunks[-1][4] + chunks[-1][3] if chunks else 0
        chunks += [(k_src, vt_src, r, step, base + r) for r in range(0, n, step)]

    def stage(score_head, sw_ref, pv_head, sr_ref, m):
        if score_head is not None:
            qt = qt_ref[0, score_head]
            mpart = jnp.full((sub, tq), -jnp.inf, F32)
        if pv_head is not None:
            lpart = jnp.zeros((sub, tq), F32)
            acc = jnp.zeros((V_HEAD, tq), F32)
        for k_ref, vt_ref, r0, n, s0 in chunks:
            rows = slice(s0, s0 + n)
            if score_head is not None:
                s = _dot(k_ref[0, score_head, r0:r0 + n, :], qt)
                sw_ref[rows, :] = s
                mpart = jnp.maximum(mpart, jnp.max(s.reshape(n // sub, sub, tq), axis=0))
            if pv_head is not None:
                p = jnp.exp2(sr_ref[rows, :] - m)
                lpart = lpart + jnp.sum(p.reshape(n // sub, sub, tq), axis=0)
                acc = acc + _dot(vt_ref[0, pv_head, :, r0:r0 + n], p.astype(BF16))
        if pv_head is not None:
            denom = jnp.sum(lpart, axis=0, keepdims=True)
            ot_ref[pl.ds(pl.multiple_of(pv_head * V_HEAD, V_HEAD), V_HEAD), :] = acc / denom
        if score_head is not None:
            return jnp.max(mpart, axis=0, keepdims=True)
        return None

    def pair(i, m0):
        h0 = 2 * i
        m1 = stage(h0 + 1, s1_ref, h0, s0_ref, m0)
        return stage(h0 + 2, s0_ref, h0 + 1, s1_ref, m1)

    m0 = lax.fori_loop(0, MLA_HEADS // 2 - 1, pair, stage(0, s0_ref, None, None, None))
    m1 = stage(MLA_HEADS - 1, s1_ref, MLA_HEADS - 2, s0_ref, m0)
    stage(None, None, MLA_HEADS - 1, s1_ref, m1)
    o_ref[0] = ot_ref[...].T.astype(BF16)


def _attention(qt, k, vt, k_ctx, vt_ctx, tq, kc):
    b, h, _, seq = qt.shape
    lc = k_ctx.shape[2]
    lk = lc + seq
    assert seq % tq == 0

    def per_batch(shape):
        return pl.BlockSpec((1,) + tuple(shape[1:]), lambda i, j: (i, 0, 0, 0))

    return pl.pallas_call(
        functools.partial(_attn_kernel, kc),
        grid=(b, seq // tq),
        in_specs=[pl.BlockSpec((1, h, HEAD_PAD, tq), lambda i, j: (i, 0, 0, j)),
                  per_batch(k.shape), per_batch(vt.shape), per_batch(k_ctx.shape), per_batch(vt_ctx.shape)],
        out_specs=pl.BlockSpec((1, tq, h * V_HEAD), lambda i, j: (i, j, 0)),
        out_shape=jax.ShapeDtypeStruct((b, seq, h * V_HEAD), BF16),
        scratch_shapes=[pltpu.VMEM((lk, tq), F32), pltpu.VMEM((lk, tq), F32), pltpu.VMEM((h * V_HEAD, tq), F32)],
        compiler_params=_params(2),
        name="mla_attention",
    )(qt, k, vt, k_ctx, vt_ctx)


def _halo_specs(tl, d, seq):
    r = tl // HALO
    last = seq // HALO - 1

    def prev_map(i, j):
        return (i, jnp.maximum(j * r - 1, 0), 0)

    def next_map(i, j):
        return (i, jnp.minimum((j + 1) * r, last), 0)

    return [pl.BlockSpec((1, HALO, d), prev_map),
            pl.BlockSpec((1, tl, d), lambda i, j: (i, j, 0)),
            pl.BlockSpec((1, HALO, d), next_map)]


def _conv_rows_valid(tl):
    n_rows = tl + 2 * CONV_PAD
    j = pl.program_id(1)
    row = lax.broadcasted_iota(jnp.int32, (n_rows, 1), 0)
    first = j == 0
    last = j == pl.num_programs(1) - 1
    return jnp.logical_not((first & (row < CONV_PAD)) | (last & (row >= n_rows - CONV_PAD)))


def _conv_rows(a):
    return a[HALO - CONV_PAD:a.shape[0] - HALO + CONV_PAD]


def _dwconv3_taps(z_ref, w0, w1, w2, tl):
    z = z_ref[...]
    n_rows = z.shape[0]
    pad = (n_rows - tl) // 2
    zm = pltpu.roll(z, 1, axis=0)[pad:pad + tl]
    zp = pltpu.roll(z, n_rows - 1, axis=0)[pad:pad + tl]
    return zm * w0 + z[pad:pad + tl] * w1 + zp * w2


def _dwconv3(z_ref, w_ref, b_ref, lo, hi, tl):
    return _dwconv3_taps(z_ref, w_ref[0:1, lo:hi], w_ref[1:2, lo:hi], w_ref[2:3, lo:hi], tl) + b_ref[0:1, lo:hi]


def _ffn_kernel(fc, final,
                xp_ref, xm_ref, xn_ref, mp_ref, mm_ref, mn_ref, mod_ref, wproj_ref, bproj_ref, gffn_ref,
                wup_ref, cw_ref, cb_ref, wdown_ref, gfin_ref, o_ref, z_ref, act_ref):
    tl = xm_ref.shape[1]
    ff = wdown_ref.shape[1]
    g1 = mod_ref[0, 0:1, :]
    sh2 = mod_ref[0, 1:2, :]
    sc2 = mod_ref[0, 2:3, :]
    g2 = mod_ref[0, 3:4, :]
    xc = jnp.concatenate([xp_ref[0], xm_ref[0], xn_ref[0]], axis=0)
    mix = jnp.concatenate([mp_ref[0], mm_ref[0], mn_ref[0]], axis=0)
    n_conv = tl + 2 * CONV_PAD
    valid = _conv_rows_valid(tl)
    off = HALO - CONV_PAD

    def prologue(lo, hi):
        t_lo = (lo + off) // HALO * HALO
        t_hi = -(-(hi + off) // HALO) * HALO
        res = xc[t_lo:t_hi] + g1 * (_dot(mix[t_lo:t_hi], wproj_ref[...]) + bproj_ref[...])
        xmid_part = res[lo + off - t_lo:hi + off - t_lo]
        h_part = _rmsnorm(xmid_part, gffn_ref[...]) * (1.0 + sc2) + sh2
        return xmid_part, jnp.where(valid[lo:hi], h_part, 0.0).astype(BF16)

    split = (n_conv // 2 + HALO - 1) // HALO * HALO
    xmid_a, h_a = prologue(0, split)
    xmid_b, h_b = prologue(split, n_conv)
    n_chunks = ff // fc

    def up_rows(c, part, h_rows, lo_row, hi_row):
        col = part * ff + c * fc
        z_ref[c % 2, part, lo_row:hi_row, :] = _dot(h_rows, wup_ref[0, :, col:col + fc])

    for part in range(2):
        up_rows(0, part, h_a, 0, split)
    for part in range(2):
        up_rows(0, part, h_b, split, n_conv)
    xmid = jnp.concatenate([xmid_a, xmid_b], axis=0)

    def up(c):
        for part in range(2):
            up_rows(c, part, h_a, 0, split)
            up_rows(c, part, h_b, split, n_conv)

    for c in range(n_chunks):
        lo = c * fc
        if c + 1 < n_chunks:
            up(c + 1)
        a = _dwconv3(z_ref.at[c % 2, 0], cw_ref, cb_ref, lo, lo + fc, tl)
        g = _dwconv3(z_ref.at[c % 2, 1], cw_ref, cb_ref, ff + lo, ff + lo + fc, tl)
        act_ref[:, lo:lo + fc] = (g * _sigmoid(g) * a).astype(BF16)
    half = tl // 2
    for r0 in (0, half):
        out = xmid[CONV_PAD + r0:CONV_PAD + r0 + half] + g2 * _dot(act_ref[r0:r0 + half, :], wdown_ref[0])
        if final:
            out = _rmsnorm(out, gfin_ref[...])
        o_ref[0, r0:r0 + half, :] = out


def _layer_spec(shape, layer):
    nd = len(shape)
    return pl.BlockSpec((1,) + tuple(shape[1:]), lambda *_: (layer,) + (0,) * (nd - 1), pipeline_mode=pl.Buffered(1))


def _ffn_layer(x, mix, mod4, w_proj, b_proj, g_ffn, w_up, conv_w, conv_b, w_down, g_final, final, layer, tl, fc):
    b, seq, d = x.shape
    dm = mix.shape[2]
    ff = w_down.shape[1]
    assert seq % tl == 0 and tl % HALO == 0 and ff % fc == 0
    return pl.pallas_call(
        functools.partial(_ffn_kernel, fc, final),
        grid=(b, seq // tl),
        in_specs=_halo_specs(tl, d, seq) + _halo_specs(tl, dm, seq) + [
            pl.BlockSpec((1, 4, d), lambda i, j: (i, 0, 0)),
            _const_spec((dm, d)),
            _const_spec((1, d)),
            _const_spec((1, d)),
            _layer_spec(w_up.shape, layer),
            _const_spec((3, 2 * ff)),
            _const_spec((1, 2 * ff)),
            _layer_spec(w_down.shape, layer),
            _const_spec((1, d)),
        ],
        out_specs=pl.BlockSpec((1, tl, d), lambda i, j: (i, j, 0)),
        out_shape=jax.ShapeDtypeStruct((b, seq, d), F32),
        scratch_shapes=[pltpu.VMEM((2, 2, tl + 2 * CONV_PAD, fc), F32), pltpu.VMEM((tl, ff), BF16)],
        compiler_params=_params(2),
        name="mixer_out_ffn",
    )(x, x, x, mix, mix, mix, mod4, w_proj.astype(BF16), b_proj.reshape(1, d), g_ffn.reshape(1, d),
      w_up, conv_w, conv_b.reshape(1, 2 * ff), w_down, g_final.reshape(1, d))


def _beta_cos_sin(seq, rev_rows, rev_cols):
    half = seq // 2
    blk = LANES
    m = np.arange(half, dtype=np.float64)
    if rev_rows:
        m = m[::-1]
    w = np.pi * (4.0 * m[:, None] + 1.0) / (4.0 * seq)
    a1 = w * (4.0 * blk * np.arange(half // blk, dtype=np.float64))[None, :]
    a2 = w * (4.0 * np.arange(blk, dtype=np.float64) + 1.0)[None, :]
    if rev_cols:
        a1, a2 = a1[:, ::-1], a2[:, ::-1]
    ca, sa = jnp.asarray(np.cos(a1), F32), jnp.asarray(np.sin(a1), F32)
    cb, sb = jnp.asarray(np.cos(a2), F32), jnp.asarray(np.sin(a2), F32)
    cos = (ca[:, :, None] * cb[:, None, :] - sa[:, :, None] * sb[:, None, :]).reshape(half, half)
    sin = (sa[:, :, None] * cb[:, None, :] + ca[:, :, None] * sb[:, None, :]).reshape(half, half)
    return cos, sin


def _transform_matrices(seq):
    c, s = _beta_cos_sin(seq, False, False)
    cj, sj = _beta_cos_sin(seq, False, True)
    jc, js = _beta_cos_sin(seq, True, False)
    cat = jnp.concatenate
    return (cat([c, s], axis=0).astype(BF16), cat([cj, sj], axis=0).astype(BF16),
            cat([c, s], axis=1).astype(BF16), cat([js, -jc], axis=1).astype(BF16))


def _split_rows(x, slab_ref):
    half = x.shape[0] // 2
    slabs = range(x.shape[1] // LANES)
    for s in slabs:
        slab_ref[s] = x[:, s * LANES:(s + 1) * LANES]
    even = jnp.concatenate([slab_ref[s, pl.ds(0, half, stride=2), :] for s in slabs], axis=1)
    odd = jnp.concatenate([slab_ref[s, pl.ds(1, half, stride=2), :] for s in slabs], axis=1)
    return even, odd


def _forward_transform(even, odd, fwd_even_ref, fwd_odd_ref):
    half = even.shape[0]
    a = _dot(fwd_even_ref[...], even.astype(BF16))
    b = _dot(fwd_odd_ref[...], odd.astype(BF16))
    a1, a2, b1, b2 = a[:half], a[half:], b[:half], b[half:]
    return a1 + b2, a2 + b1, a2 - b1, a1 - b2


def _hyena_pos_features(seq):
    t = jnp.linspace(0.0, 1.0, seq, dtype=F32)
    w = 2.0 * math.pi * jnp.arange(seq, dtype=F32) / seq
    f = jnp.linspace(1e-4, HY_BANDS - 1, HY_BANDS, dtype=F32)
    z = jnp.concatenate([t[:, None], jnp.cos(w[:, None] * f), -jnp.sin(w[:, None] * f)], axis=-1)
    return t, z


def _hyena_filter_kernel(z_ref, t_ref, ph_ref, w1_ref, b1_ref, f1_ref, w2_ref, b2_ref, f2_ref,
                         w3f_ref, w3b_ref, decay_ref, fwd_even_ref, fwd_odd_ref,
                         her_ref, hei_ref, hor_ref, hoi_ref, hm_ref, slab_ref):
    seq = z_ref.shape[0]

    @pl.when(pl.program_id(0) == 0)
    def _():
        h1 = jnp.sin(f1_ref[...] * (_dot3(z_ref[...], w1_ref[...]) + b1_ref[...]))
        hm_ref[...] = jnp.sin(f2_ref[...] * (_dot3(h1, w2_ref[...]) + b2_ref[...]))

    hm = hm_ref[...]
    window = jnp.exp(-t_ref[...] * jnp.abs(decay_ref[...]))
    h_fwd = _dot3(hm, w3f_ref[...]) * window
    h_bwd = _dot3(hm, w3b_ref[...]) * window
    row = lax.broadcasted_iota(jnp.int32, (seq, 1), 0)
    h_bwd = jnp.where(row == 0, 0.0, h_bwd)
    pp = _forward_transform(*_split_rows(h_fwd + h_bwd, slab_ref), fwd_even_ref, fwd_odd_ref)
    qq = _forward_transform(*_split_rows(h_fwd - h_bwd, slab_ref), fwd_even_ref, fwd_odd_ref)
    ce, se, co, so = ph_ref[:, 0:1], ph_ref[:, 1:2], ph_ref[:, 2:3], ph_ref[:, 3:4]
    her_ref[...] = ce * pp[0] + se * pp[1]
    hei_ref[...] = se * qq[0] - ce * qq[1]
    hor_ref[...] = co * pp[2] + so * pp[3]
    hoi_ref[...] = so * qq[2] - co * qq[3]


def _hyena_spectra(seq, f_w1, f_b1, f_freq1, f_w2, f_b2, f_freq2, f_w3, decay, fwd_even, fwd_odd, dc):
    d = decay.shape[0]
    nh = f_w1.shape[1]
    half = seq // 2
    t, z = _hyena_pos_features(seq)
    z = _pad_cols(z, LANES)
    f_w1 = jnp.pad(f_w1, ((0, LANES - f_w1.shape[0]), (0, 0)))
    n = 2 * seq
    m = np.arange(half, dtype=np.float64)
    phi_e = np.pi * (4.0 * m + 1.0) / (4.0 * seq)
    phi_o = np.pi * (2.0 * (seq - 1 - 2.0 * m) + 1.0) / (4.0 * seq)
    phase = jnp.asarray(np.stack([np.cos(phi_e), np.sin(phi_e), np.cos(phi_o), np.sin(phi_o)], axis=1) * (2.0 / n), F32)
    nd = d // dc
    out_spec = pl.BlockSpec((half, dc), lambda j: (0, j))
    return pl.pallas_call(
        _hyena_filter_kernel,
        grid=(nd,),
        in_specs=[_const_spec(z.shape), _const_spec((seq, 1)), _const_spec((half, 4)),
                  _const_spec(f_w1.shape), _const_spec((1, nh)), _const_spec((1, nh)),
                  _const_spec(f_w2.shape), _const_spec((1, nh)), _const_spec((1, nh)),
                  pl.BlockSpec((nh, dc), lambda j: (0, j)),
                  pl.BlockSpec((nh, dc), lambda j: (0, nd + j)),
                  pl.BlockSpec((1, dc), lambda j: (0, j)),
                  _const_spec(fwd_even.shape), _const_spec(fwd_odd.shape)],
        out_specs=[out_spec] * 4,
        out_shape=[jax.ShapeDtypeStruct((half, d), F32)] * 4,
        scratch_shapes=[pltpu.VMEM((seq, nh), F32), pltpu.VMEM((dc // LANES, seq, LANES), F32)],
        compiler_params=_params(1),
        name="hyena_spectra",
    )(z, t.reshape(seq, 1), phase, f_w1, f_b1.reshape(1, nh), f_freq1.reshape(1, nh), f_w2, f_b2.reshape(1, nh),
      f_freq2.reshape(1, nh), f_w3, f_w3, decay.reshape(1, d), fwd_even, fwd_odd)


def _hyena_in_kernel(dc, xp_ref, xm_ref, xn_ref, mod_ref, gmix_ref, win_ref, bin_ref, cw_ref, cb_ref,
                     vv_ref, x1_ref, z_ref):
    tl = xm_ref.shape[1]
    d = xm_ref.shape[2]
    shift = mod_ref[0, 0:1, :]
    scale = mod_ref[0, 1:2, :]
    xc = jnp.concatenate([xp_ref[0][HALO - CONV_PAD:], xm_ref[0], xn_ref[0][:CONV_PAD]], axis=0)
    h = _rmsnorm(xc, gmix_ref[...]) * (1.0 + scale) + shift
    h = jnp.where(_conv_rows_valid(tl), h, 0.0).astype(BF16)
    n_tiles = d // dc
    j = pl.program_id(1)
    row8 = lax.broadcasted_iota(jnp.int32, (CONV_PAD, 1), 0)
    seq_start = ((j == 0) & (row8 == 0)).astype(F32)
    seq_end = ((j == pl.num_programs(1) - 1) & (row8 == CONV_PAD - 1)).astype(F32)

    def project(c, part):
        lo = part * d + c * dc
        z_ref[c % 2, part] = _dot(h, win_ref[:, lo:lo + dc])

    def conv(c, part):
        lo = part * d + c * dc
        w0, w1, w2 = cw_ref[0:1, lo:lo + dc], cw_ref[1:2, lo:lo + dc], cw_ref[2:3, lo:lo + dc]
        b_in = bin_ref[0:1, lo:lo + dc]
        out = _dwconv3_taps(z_ref.at[c % 2, part], w0, w1, w2, tl) + (cb_ref[0:1, lo:lo + dc] + b_in * (w0 + w1 + w2))
        top = out[:CONV_PAD] - seq_start * (b_in * w0)
        bottom = out[tl - CONV_PAD:] - seq_end * (b_in * w2)
        return jnp.concatenate([top, out[CONV_PAD:tl - CONV_PAD], bottom], axis=0)

    units = [(c, part) for c in range(n_tiles) for part in range(3)]
    project(*units[0])
    done = {}
    for i, (c, part) in enumerate(units):
        if i + 1 < len(units):
            project(*units[i + 1])
        done[part] = conv(c, part)
        if part == 2:
            x1, x2, v = done[0], done[1], done[2]
            vv_ref[0, :, c * dc:(c + 1) * dc] = (v * x2).astype(BF16)
            x1_ref[0, :, c * dc:(c + 1) * dc] = x1.astype(BF16)


def _hyena_in(x, mod2, g_mix, w_in, b_in, conv_w, conv_b, tl, dc):
    b, seq, d = x.shape
    assert seq % tl == 0 and d % dc == 0
    out_spec = pl.BlockSpec((1, tl, d), lambda i, j: (i, j, 0))
    return pl.pallas_call(
        functools.partial(_hyena_in_kernel, dc),
        grid=(b, seq // tl),
        in_specs=_halo_specs(tl, d, seq) + [
            pl.BlockSpec((1, 2, d), lambda i, j: (i, 0, 0)),
            _const_spec((1, d)),
            _const_spec((d, 3 * d)),
            _const_spec((1, 3 * d)),
            _const_spec((3, 3 * d)),
            _const_spec((1, 3 * d)),
        ],
        out_specs=[out_spec, out_spec],
        out_shape=[jax.ShapeDtypeStruct((b, seq, d), BF16), jax.ShapeDtypeStruct((b, seq, d), BF16)],
        scratch_shapes=[pltpu.VMEM((2, 3, tl + 2 * CONV_PAD, dc), F32)],
        compiler_params=_params(2),
        name="hyena_in",
    )(x, x, x, mod2, g_mix.reshape(1, d), w_in.astype(BF16), b_in.reshape(1, 3 * d), conv_w, conv_b.reshape(1, 3 * d))


def _hyena_conv_kernel(vv_ref, x1_ref, her_ref, hei_ref, hor_ref, hoi_ref, dbias_ref,
                       fwd_even_ref, fwd_odd_ref, inv_even_ref, inv_odd_ref, o_ref, slab_ref):
    seq = vv_ref.shape[1]
    half = seq // 2
    slabs = range(vv_ref.shape[2] // LANES)
    v_even, v_odd = _split_rows(vv_ref[0].astype(F32), slab_ref)
    p1, p2, p3, p4 = _forward_transform(v_even, v_odd, fwd_even_ref, fwd_odd_ref)
    her, hei, hor, hoi = her_ref[...], hei_ref[...], hor_ref[...], hoi_ref[...]
    yer = p1 * her + p2 * hei
    yei = p1 * hei - p2 * her
    yor = p3 * hor + p4 * hoi
    yoi = p3 * hoi - p4 * hor
    y_even = _dot(inv_even_ref[...], jnp.concatenate([yer - yoi, yor - yei], axis=0).astype(BF16))
    y_odd = _dot(inv_odd_ref[...], jnp.concatenate([yer + yoi, yei + yor], axis=0).astype(BF16))
    x1_even, x1_odd = _split_rows(x1_ref[0].astype(F32), slab_ref)
    dbias = dbias_ref[...]
    out_even = (y_even + v_even * dbias) * x1_even
    out_odd = (y_odd + v_odd * dbias) * x1_odd
    for s in slabs:
        slab_ref[s, pl.ds(0, half, stride=2), :] = out_even[:, s * LANES:(s + 1) * LANES]
        slab_ref[s, pl.ds(1, half, stride=2), :] = out_odd[:, s * LANES:(s + 1) * LANES]
    o_ref[0] = jnp.concatenate([slab_ref[s] for s in slabs], axis=1).astype(BF16)


def _hyena_conv(vv, x1, spectra, d_bias, matrices, dc):
    b, seq, d = vv.shape
    half = seq // 2
    blk = pl.BlockSpec((1, seq, dc), lambda i, j: (i, 0, j))
    spec = pl.BlockSpec((half, dc), lambda i, j: (0, j))
    return pl.pallas_call(
        _hyena_conv_kernel,
        grid=(b, d // dc),
        in_specs=[blk, blk] + [spec] * 4 + [pl.BlockSpec((1, dc), lambda i, j: (0, j))]
                 + [_const_spec(m.shape) for m in matrices],
        out_specs=blk,
        out_shape=jax.ShapeDtypeStruct((b, seq, d), BF16),
        scratch_shapes=[pltpu.VMEM((dc // LANES, seq, LANES), F32)],
        compiler_params=_params(2),
        name="hyena_conv",
    )(vv, x1, *spectra, d_bias.reshape(1, d), *matrices)


def _tiles(seq, lc):
    ctx_tl = math.gcd(MXU_DIM, math.gcd(seq, lc))
    return dict(proj_tl=min(2 * MXU_DIM, seq), ctx_tl=ctx_tl, attn_tq=min(2 * MXU_DIM, seq), attn_kc=MXU_DIM,
                ffn_tl=min(2 * MXU_DIM, seq), ffn_fc=MXU_DIM, hy_tl=min(4 * MXU_DIM, seq), hy_dc=MXU_DIM)


def kernel(x, c, ctx, c_ctx, mod_w, mod_b, norm_mix_g, norm_ffn_g, mla_w_dq, mla_g_q, mla_w_uq, mla_w_dkv, mla_g_kv, mla_w_uk, mla_w_uv, mla_w_o, hy_w_in, hy_b_in, hy_conv_w, hy_conv_b, hy_f_w1, hy_f_b1, hy_f_freq1, hy_f_w2, hy_f_b2, hy_f_freq2, hy_f_w3, hy_decay, hy_d_bias, hy_w_out, hy_b_out, ffn_w_up, ffn_conv_w, ffn_conv_b, ffn_w_down, final_g):
    b, seq, d = x.shape
    lc = ctx.shape[1]
    depth = mod_w.shape[0]
    assert depth == 2, "layer 0 is MLA, layer 1 is Hyena"
    t = _tiles(seq, lc)

    rows = -(-(b + 1) // 8) * 8
    cc = jnp.concatenate([c, c_ctx[None], jnp.zeros((rows - b - 1, d), F32)], axis=0)
    mods, w_up, w_down = _modulation(cc, mod_w, mod_b, ffn_w_up, ffn_w_down)
    mods = mods.reshape(depth, rows, 6, d)

    m0 = mods[0]
    qkv = _mla_projections(x, ctx, m0[:b, 0:2], m0[b, 0:2], norm_mix_g[0], mla_w_dq[0], mla_g_q[0], mla_w_uq[0],
                           mla_w_dkv[0], mla_g_kv[0], mla_w_uk[0], mla_w_uv[0], t["proj_tl"], t["ctx_tl"])
    ox = _attention(*qkv, t["attn_tq"], t["attn_kc"])
    x = _ffn_layer(x, ox, m0[:b, 2:6], mla_w_o[0], jnp.zeros((d,), F32), norm_ffn_g[0], w_up, ffn_conv_w[0],
                   ffn_conv_b[0], w_down, final_g, False, 0, t["ffn_tl"], t["ffn_fc"])

    m1 = mods[1]
    matrices = _transform_matrices(seq)
    spectra = _hyena_spectra(seq, hy_f_w1[0], hy_f_b1[0], hy_f_freq1[0], hy_f_w2[0], hy_f_b2[0], hy_f_freq2[0],
                             hy_f_w3[0], hy_decay[0], matrices[0], matrices[1], t["hy_dc"])
    vv, x1 = _hyena_in(x, m1[:b, 0:2], norm_mix_g[1], hy_w_in[0], hy_b_in[0], hy_conv_w[0], hy_conv_b[0],
                       t["hy_tl"], t["hy_dc"])
    yy = _hyena_conv(vv, x1, spectra, hy_d_bias[0], matrices, t["hy_dc"])
    return _ffn_layer(x, yy, m1[:b, 2:6], hy_w_out[0], hy_b_out[0], norm_ffn_g[1], w_up, ffn_conv_w[1],
                      ffn_conv_b[1], w_down, final_g, True, 1, t["ffn_tl"], t["ffn_fc"])
```
